```python
import jax, jax.numpy as jnp
from jax import lax
import numpy as np

D_MODEL = 1024
BATCH = 8
SEQ = 2048
DEPTH = 2

D_MIX = D_MODEL
RET_HEADS = 8
RET_HEAD_DIM = 64
RET_WIDTH = RET_HEADS * RET_HEAD_DIM
RET_CHUNK = 128
ROPE_BASE = 10000.0
DSA_HEADS = 8
DSA_HEAD_DIM = 64
DSA_WIDTH = DSA_HEADS * DSA_HEAD_DIM
DSA_LATENT = 128
IDX_HEADS = 8
IDX_DIM = 64
IDX_TOPK_MAX = 256
Q_BLOCK = 128
PLE_DIM = 256
D_FF = -(-8 * D_MODEL // (3 * 256)) * 256
EPS = 1e-6
IN_COLS = 4 * RET_WIDTH + DSA_WIDTH + DSA_LATENT + IDX_HEADS * IDX_DIM + IDX_DIM + IDX_HEADS

kernel_name = "hymba_style_retention_dsa_hybrid"


def rms_norm(x, g):
    xf = x.astype(jnp.float32)
    y = xf * lax.rsqrt(jnp.mean(xf * xf, axis=-1, keepdims=True) + EPS)
    return (y * g.astype(jnp.float32)).astype(x.dtype)


def layer_norm(x, g, b):
    xf = x.astype(jnp.float32)
    mu = jnp.mean(xf, axis=-1, keepdims=True)
    var = jnp.mean(jnp.square(xf - mu), axis=-1, keepdims=True)
    y = (xf - mu) * lax.rsqrt(var + EPS)
    return (y * g.astype(jnp.float32) + b.astype(jnp.float32)).astype(x.dtype)


def rotate(x, positions):
    d = x.shape[-1]
    inv_freq = 1.0 / (ROPE_BASE ** (jnp.arange(0, d, 2, dtype=jnp.float32) / d))
    ang = positions.astype(jnp.float32)[:, :, None] * inv_freq
    cos = jnp.cos(ang)[:, :, None, :]
    sin = jnp.sin(ang)[:, :, None, :]
    xf = x.astype(jnp.float32)
    x1, x2 = xf[..., : d // 2], xf[..., d // 2:]
    return jnp.concatenate([x1 * cos - x2 * sin, x1 * sin + x2 * cos], axis=-1)


def retention_chunkwise(q, k, v):
    B, S, H, d = q.shape
    C = RET_CHUNK
    N = S // C
    log_g = jnp.log1p(-jnp.power(2.0, -5.0 - jnp.arange(H, dtype=jnp.float32)))
    q = q.astype(jnp.float32).reshape(B, N, C, H, d)
    k = (k.astype(jnp.float32) * (d ** -0.5)).reshape(B, N, C, H, d)
    v = v.astype(jnp.float32).reshape(B, N, C, H, d)
    i = jnp.arange(C, dtype=jnp.float32)
    diff = i[:, None] - i[None, :]
    decay = jnp.where(diff >= 0, jnp.exp(log_g[:, None, None] * jnp.maximum(diff, 0.0)), 0.0)
    scores = jnp.einsum('bnihd,bnjhd->bnhij', q, k) * decay
    intra = jnp.einsum('bnhij,bnjhe->bnihe', scores, v)
    zeta = jnp.exp(log_g[None, :] * (C - 1 - i)[:, None])
    kv = jnp.einsum('bnjhd,bnjhe->bnhde', k * zeta[None, None, :, :, None], v)
    chunk_decay = jnp.exp(log_g * C)[None, :, None, None]

    def step(R, kv_n):
        return chunk_decay * R + kv_n, R

    _, R_prev = lax.scan(step, jnp.zeros((B, H, d, d), jnp.float32), jnp.moveaxis(kv, 1, 0))
    R_prev = jnp.moveaxis(R_prev, 0, 1)
    xi = jnp.exp(log_g[None, :] * (i + 1.0)[:, None])
    cross = jnp.einsum('bnihd,bnhde->bnihe', q * xi[None, None, :, :, None], R_prev)
    return (intra + cross).reshape(B, S, H, d)


def dsa_sparse_attention(q_lat, c_kv, qi, ki, wi):
    B, S, H, C = q_lat.shape
    NB = S // Q_BLOCK
    topk = min(IDX_TOPK_MAX, S // 4)
    s_idx = jnp.arange(S)
    b_idx = jnp.arange(B)[:, None, None]

    def to_blocks(a):
        return jnp.moveaxis(a.reshape((B, NB, Q_BLOCK) + a.shape[2:]), 1, 0)

    def block(args):
        ql_b, qi_b, wi_b, start = args
        t = start + jnp.arange(Q_BLOCK)
        rel = jax.nn.relu(jnp.einsum('bthd,bsd->bths', qi_b, ki).astype(jnp.float32))
        iscore = jnp.einsum('bths,bth->bts', rel, wi_b.astype(jnp.float32))
        causal = s_idx[None, :] <= t[:, None]
        iscore = jnp.where(causal[None], iscore, -jnp.inf)
        _, sel = lax.top_k(iscore, topk)
        valid = sel <= t[None, :, None]
        kv_sel = c_kv[b_idx, sel]
        logits = jnp.einsum('bthc,btkc->bthk', ql_b, kv_sel).astype(jnp.float32)
        logits = jnp.where(valid[:, :, None, :], logits, -jnp.inf)
        probs = jax.nn.softmax(logits, axis=-1).astype(kv_sel.dtype)
        return jnp.einsum('bthk,btkc->bthc', probs, kv_sel)

    starts = jnp.arange(NB, dtype=jnp.int32) * Q_BLOCK
    o = lax.map(block, (to_blocks(q_lat), to_blocks(qi), to_blocks(wi), starts))
    return jnp.moveaxis(o, 0, 1).reshape(B, S, H, C)


def split_cols(proj):
    sizes = (RET_WIDTH,) * 4 + (DSA_WIDTH, DSA_LATENT, IDX_HEADS * IDX_DIM, IDX_DIM, IDX_HEADS)
    out, o = [], 0
    for sz in sizes:
        out.append(proj[..., o:o + sz])
        o += sz
    return out


def setup_inputs(seed: int = 0) -> dict:
    key = jax.random.key(seed)
    ks = jax.random.split(key, 24)
    f32 = jnp.float32

    def nrm(k, shape, fan_in):
        return jax.random.normal(k, shape, f32) * (fan_in ** -0.5)

    def gain(k, shape):
        return 1.0 + 0.02 * jax.random.normal(k, shape, f32)

    def bias(k, shape):
        return 0.02 * jax.random.normal(k, shape, f32)

    L = DEPTH
    x = jax.random.normal(ks[0], (BATCH, SEQ, D_MODEL), f32)
    p = jax.random.normal(ks[1], (DEPTH, BATCH, SEQ, PLE_DIM), f32)
    offsets = jax.random.randint(ks[2], (BATCH, 1), 0, 1024, dtype=jnp.int32)
    positions = offsets + jnp.arange(SEQ, dtype=jnp.int32)[None, :]
    return {
        "x": x,
        "p": p,
        "positions": positions,
        "pre_mix_g": gain(ks[3], (L, D_MODEL)),
        "w_in": nrm(ks[4], (L, D_MODEL, IN_COLS), D_MODEL),
        "ret_norm_g": gain(ks[5], (L, RET_HEADS, RET_HEAD_DIM)),
        "ret_norm_b": bias(ks[6], (L, RET_HEADS, RET_HEAD_DIM)),
        "kv_norm_g": gain(ks[7], (L, DSA_LATENT)),
        "idx_k_norm_g": gain(ks[8], (L, IDX_DIM)),
        "idx_k_norm_b": bias(ks[9], (L, IDX_DIM)),
        "w_uk": nrm(ks[10], (L, DSA_HEADS, DSA_HEAD_DIM, DSA_LATENT), DSA_HEAD_DIM),
        "w_uv": nrm(ks[11], (L, DSA_HEADS, DSA_LATENT, DSA_HEAD_DIM), DSA_LATENT),
        "w_out": nrm(ks[12], (L, D_MIX, D_MODEL), D_MIX),
        "post_mix_g": gain(ks[13], (L, D_MODEL)),
        "pre_ffn_g": gain(ks[14], (L, D_MODEL)),
        "w_ffn_gate": nrm(ks[15], (L, D_MODEL, D_FF), D_MODEL),
        "w_ffn_up": nrm(ks[16], (L, D_MODEL, D_FF), D_MODEL),
        "w_ffn_down": nrm(ks[17], (L, D_FF, D_MODEL), D_FF),
        "post_ffn_g": gain(ks[18], (L, D_MODEL)),
        "w_ple_up": nrm(ks[19], (L, PLE_DIM, D_MODEL), PLE_DIM),
        "w_ple_gate": nrm(ks[20], (L, D_MODEL, D_MODEL), D_MODEL),
        "ple_norm_g": gain(ks[21], (L, D_MODEL)),
    }


def reference(x, p, positions, pre_mix_g, w_in, ret_norm_g, ret_norm_b, kv_norm_g,
              idx_k_norm_g, idx_k_norm_b, w_uk, w_uv, w_out, post_mix_g, pre_ffn_g,
              w_ffn_gate, w_ffn_up, w_ffn_down, post_ffn_g, w_ple_up, w_ple_gate, ple_norm_g):
    B, S, _ = x.shape
    for i in range(DEPTH):
        h = rms_norm(x, pre_mix_g[i])
        proj = h @ w_in[i]
        r_q, r_k, r_v, r_g, a_q, a_c, i_q, i_k, i_w = split_cols(proj)

        rq = rotate(r_q.reshape(B, S, RET_HEADS, RET_HEAD_DIM), positions)
        rk = rotate(r_k.reshape(B, S, RET_HEADS, RET_HEAD_DIM), positions)
        rv = r_v.reshape(B, S, RET_HEADS, RET_HEAD_DIM)
        ry = retention_chunkwise(rq, rk, rv).astype(x.dtype)
        ry = layer_norm(ry, ret_norm_g[i], ret_norm_b[i]).reshape(B, S, RET_WIDTH)
        y_ret = jax.nn.silu(r_g) * ry

        aq = a_q.reshape(B, S, DSA_HEADS, DSA_HEAD_DIM)
        c_kv = rms_norm(a_c, kv_norm_g[i])
        q_lat = jnp.einsum('bshd,hdc->bshc', aq, w_uk[i]) * (DSA_HEAD_DIM ** -0.5)
        qi = i_q.reshape(B, S, IDX_HEADS, IDX_DIM)
        ki = layer_norm(i_k, idx_k_norm_g[i], idx_k_norm_b[i])
        wi = i_w * ((IDX_HEADS ** -0.5) * (IDX_DIM ** -0.5))
        o_lat = dsa_sparse_attention(q_lat, c_kv, qi, ki, wi)
        y_dsa = jnp.einsum('bshc,hcd->bshd', o_lat, w_uv[i]).reshape(B, S, DSA_WIDTH)

        mix = jnp.concatenate([y_ret, y_dsa], axis=-1) @ w_out[i]
        x = x + rms_norm(mix, post_mix_g[i])

        f = rms_norm(x, pre_ffn_g[i])
        f = (jax.nn.silu(f @ w_ffn_gate[i]) * (f @ w_ffn_up[i])) @ w_ffn_down[i]
        x = x + rms_norm(f, post_ffn_g[i])

        e = p[i] @ w_ple_up[i]
        gate = jax.nn.sigmoid(x @ w_ple_gate[i])
        x = x + rms_norm(gate * e, ple_norm_g[i])
    return x
```

```python
import functools

import jax
import jax.numpy as jnp
from jax import lax
from jax.experimental import pallas as pl
from jax.experimental.pallas import tpu as pltpu

D_MODEL = 1024
RET_HEADS = 8
RET_HEAD_DIM = 64
RET_WIDTH = RET_HEADS * RET_HEAD_DIM
RET_CHUNK = 128
ROPE_BASE = 10000.0
DSA_HEADS = 8
DSA_HEAD_DIM = 64
DSA_WIDTH = DSA_HEADS * DSA_HEAD_DIM
DSA_LATENT = 128
IDX_HEADS = 8
IDX_DIM = 64
IDX_TOPK_MAX = 256
Q_BLOCK = 128
PLE_DIM = 256
D_FF = 2816
EPS = 1e-6

LANES = 128
VMEM_LIMIT = 56 * 1024 * 1024
INT_MIN = -(2 ** 31)
NEG_BIG = -1e30

F32 = jnp.float32
BF16 = jnp.bfloat16

_OFF_RET = 0
_OFF_AQ = 4 * RET_WIDTH
_OFF_AC = _OFF_AQ + DSA_WIDTH
_OFF_IQ = _OFF_AC + DSA_LATENT
_OFF_IK = _OFF_IQ + IDX_HEADS * IDX_DIM
_OFF_IW = _OFF_IK + IDX_DIM
_TOK_COLS = 4 * RET_WIDTH + DSA_LATENT + LANES


def _nt_dot(a, b):
    return lax.dot_general(a, b, (((1,), (1,)), ((), ())), preferred_element_type=F32)


def _dot(a, b):
    return jnp.dot(a, b, preferred_element_type=F32)


def _const_spec(shape):
    return pl.BlockSpec(shape, lambda *_: (0,) * len(shape))


def _rope_kernel(pos_ref, freq_ref, sign_ref, cos_ref, sin_ref):
    ang = pos_ref[...].astype(F32) * freq_ref[...]
    cos_ref[...] = jnp.cos(ang)
    sin_ref[...] = jnp.sin(ang) * sign_ref[...]


def _rope_tables(positions):
    t = positions.size
    tm = min(t, 1024)
    d = RET_HEAD_DIM
    inv_freq = 1.0 / (ROPE_BASE ** (jnp.arange(0, d, 2, dtype=F32) / d))
    freq = jnp.tile(inv_freq, LANES // (d // 2)).reshape(1, LANES)
    sign = jnp.tile(jnp.concatenate([-jnp.ones(d // 2, F32), jnp.ones(d // 2, F32)]), LANES // d).reshape(1, LANES)
    return pl.pallas_call(
        _rope_kernel,
        grid=(t // tm,),
        in_specs=[pl.BlockSpec((tm, 1), lambda i: (i, 0)), _const_spec((1, LANES)), _const_spec((1, LANES))],
        out_specs=[pl.BlockSpec((tm, LANES), lambda i: (i, 0))] * 2,
        out_shape=[jax.ShapeDtypeStruct((t, LANES), F32)] * 2,
        name="rope_tables",
    )(positions.reshape(t, 1), freq, sign)


def _proj_kernel(x_ref, g_ref, wt_ref, wf_ref, wfw_ref, cos_ref, sin_ref, kvg_ref, ikg_ref, ikb_ref, wukt_ref,
                 rq_ref, rk_ref, rv_ref, rg_ref, ckv_ref, ckvt_ref, ki_ref, qit_ref, wit_ref, qlt_ref):
    tm = x_ref.shape[0]
    x = x_ref[...]
    h = (x * lax.rsqrt(jnp.mean(x * x, axis=-1, keepdims=True) + EPS) * g_ref[...]).astype(BF16)
    pt = _dot(h, wt_ref[...])
    pf = _nt_dot(wf_ref[...], h)
    pw = _nt_dot(wfw_ref[...], h)

    cos = cos_ref[...]
    sin = sin_ref[...]
    lane = lax.broadcasted_iota(jnp.int32, (tm, LANES), 1)
    first_half = (lane % RET_HEAD_DIM) < (RET_HEAD_DIM // 2)

    def rotate(xg):
        partner = jnp.where(first_half, pltpu.roll(xg, LANES - RET_HEAD_DIM // 2, 1),
                            pltpu.roll(xg, RET_HEAD_DIM // 2, 1))
        return xg * cos + partner * sin

    for g in range(RET_WIDTH // LANES):
        sl = slice(g * LANES, (g + 1) * LANES)
        rq_ref[:, sl] = rotate(pt[:, g * LANES:(g + 1) * LANES]).astype(BF16)
        rk = rotate(pt[:, RET_WIDTH + g * LANES:RET_WIDTH + (g + 1) * LANES])
        rk_ref[:, sl] = (rk * (RET_HEAD_DIM ** -0.5)).astype(BF16)
    rv_ref[...] = pt[:, 2 * RET_WIDTH:3 * RET_WIDTH].astype(BF16)
    rg = pt[:, 3 * RET_WIDTH:4 * RET_WIDTH]
    rg_ref[...] = (rg * jax.nn.sigmoid(rg)).astype(BF16)

    ac = pt[:, 4 * RET_WIDTH:4 * RET_WIDTH + DSA_LATENT]
    ckv = ac * lax.rsqrt(jnp.mean(ac * ac, axis=-1, keepdims=True) + EPS) * kvg_ref[...]
    ckv_ref[...] = ckv.astype(BF16)
    for c in range(tm // LANES):
        ckvt_ref[c] = ckv[c * LANES:(c + 1) * LANES, :].T.astype(BF16)

    ik = pt[:, 4 * RET_WIDTH + DSA_LATENT:4 * RET_WIDTH + DSA_LATENT + IDX_DIM]
    mu = jnp.mean(ik, axis=-1, keepdims=True)
    var = jnp.mean(jnp.square(ik - mu), axis=-1, keepdims=True)
    ki_ref[...] = ((ik - mu) * lax.rsqrt(var + EPS) * ikg_ref[...] + ikb_ref[...]).astype(BF16)

    qit_ref[...] = pf[DSA_WIDTH:, :].astype(BF16)
    wit_ref[...] = pw[:IDX_HEADS, :] * ((IDX_HEADS ** -0.5) * (IDX_DIM ** -0.5))
    for hh in range(DSA_HEADS):
        aqt = pf[hh * DSA_HEAD_DIM:(hh + 1) * DSA_HEAD_DIM, :].astype(BF16)
        ql = _dot(wukt_ref[hh], aqt) * (DSA_HEAD_DIM ** -0.5)
        qlt_ref[hh * DSA_LATENT:(hh + 1) * DSA_LATENT, :] = ql.astype(BF16)


def _proj_call(x2, g, wt, wf, wfw, cos, sin, kvg, ikg, ikb, wukt, tm):
    t = x2.shape[0]
    row = lambda w: pl.BlockSpec((tm, w), lambda i: (i, 0))
    col = lambda r: pl.BlockSpec((r, tm), lambda i: (0, i))
    out_shape = [
        jax.ShapeDtypeStruct((t, RET_WIDTH), BF16),
        jax.ShapeDtypeStruct((t, RET_WIDTH), BF16),
        jax.ShapeDtypeStruct((t, RET_WIDTH), BF16),
        jax.ShapeDtypeStruct((t, RET_WIDTH), BF16),
        jax.ShapeDtypeStruct((t, DSA_LATENT), BF16),
        jax.ShapeDtypeStruct((t // LANES, DSA_LATENT, LANES), BF16),
        jax.ShapeDtypeStruct((t, IDX_DIM), BF16),
        jax.ShapeDtypeStruct((IDX_HEADS * IDX_DIM, t), BF16),
        jax.ShapeDtypeStruct((IDX_HEADS, t), F32),
        jax.ShapeDtypeStruct((DSA_HEADS * DSA_LATENT, t), BF16),
    ]
    out_specs = [row(RET_WIDTH)] * 4 + [
        row(DSA_LATENT),
        pl.BlockSpec((tm // LANES, DSA_LATENT, LANES), lambda i: (i, 0, 0)),
        row(IDX_DIM),
        col(IDX_HEADS * IDX_DIM),
        col(IDX_HEADS),
        col(DSA_HEADS * DSA_LATENT),
    ]
    in_specs = [
        row(D_MODEL), _const_spec((1, D_MODEL)), _const_spec(wt.shape), _const_spec(wf.shape), _const_spec(wfw.shape),
        row(LANES), row(LANES), _const_spec((1, DSA_LATENT)), _const_spec((1, IDX_DIM)), _const_spec((1, IDX_DIM)),
        _const_spec(wukt.shape),
    ]
    return pl.pallas_call(
        _proj_kernel, grid=(t // tm,), in_specs=in_specs, out_specs=out_specs, out_shape=out_shape,
        compiler_params=pltpu.CompilerParams(dimension_semantics=("arbitrary",), vmem_limit_bytes=VMEM_LIMIT),
        name="in_proj",
    )(x2, g, wt, wf, wfw, cos, sin, kvg, ikg, ikb, wukt)


def _ret_kernel(q_ref, k_ref, v_ref, sg_ref, decay_ref, xi_ref, zeta_ref, cd_ref, ng_ref, nb_ref, o_ref):
    c = RET_CHUNK
    n_chunks = q_ref.shape[0] // c
    lane = lax.broadcasted_iota(jnp.int32, (c, LANES), 1)
    head0 = lane < RET_HEAD_DIM
    row = lax.broadcasted_iota(jnp.int32, (LANES, LANES), 0)
    col = lax.broadcasted_iota(jnp.int32, (LANES, LANES), 1)
    same_head = (row < RET_HEAD_DIM) == (col < RET_HEAD_DIM)
    d0 = decay_ref[0]
    d1 = decay_ref[1]
    xi = xi_ref[0]
    zeta = zeta_ref[0]
    cd = cd_ref[0]
    ng = ng_ref[0]
    nb = nb_ref[0]
    inv_d = 1.0 / RET_HEAD_DIM

    def head_stat(a):
        s0 = jnp.sum(jnp.where(head0, a, 0.0), axis=-1, keepdims=True)
        s1 = jnp.sum(jnp.where(head0, 0.0, a), axis=-1, keepdims=True)
        return jnp.where(head0, s0, s1) * inv_d

    def body(n, r_state):
        rows = pl.ds(pl.multiple_of(n * c, c), c)
        q = q_ref[rows, :]
        k = k_ref[rows, :]
        v = v_ref[rows, :]
        zero = jnp.zeros_like(q)
        s0 = _nt_dot(jnp.where(head0, q, zero), k) * d0
        s1 = _nt_dot(jnp.where(head0, zero, q), k) * d1
        sc = jnp.concatenate([s0, s1], axis=1).astype(BF16)
        vm = jnp.concatenate([jnp.where(head0, v, zero), jnp.where(head0, zero, v)], axis=0)
        intra = _dot(sc, vm)
        cross = _dot((q.astype(F32) * xi).astype(BF16), r_state.astype(BF16))
        o = intra + cross
        mu = head_stat(o)
        dlt = o - mu
        var = head_stat(dlt * dlt)
        y = dlt * lax.rsqrt(var + EPS) * ng + nb
        o_ref[rows, :] = (sg_ref[rows, :].astype(F32) * y).astype(BF16)
        kz = (k.astype(F32) * zeta).T.astype(BF16)
        kv = jnp.where(same_head, _dot(kz, v), 0.0)
        return cd * r_state + kv

    lax.fori_loop(0, n_chunks, body, jnp.zeros((LANES, LANES), F32))


def _ret_consts():
    h, c = RET_HEADS, RET_CHUNK
    log_g = jnp.log1p(-jnp.power(2.0, -5.0 - jnp.arange(h, dtype=F32)))
    i = jnp.arange(c, dtype=F32)
    diff = i[:, None] - i[None, :]
    decay = jnp.where(diff >= 0, jnp.exp(log_g[:, None, None] * jnp.maximum(diff, 0.0)), 0.0)
    zeta = jnp.exp(log_g[None, :] * (c - 1 - i)[:, None])
    xi = jnp.exp(log_g[None, :] * (i + 1.0)[:, None])
    cd = jnp.exp(log_g * c)
    pairs = h * RET_HEAD_DIM // LANES

    def per_lane(a):
        return jnp.repeat(a, RET_HEAD_DIM, axis=1).reshape(c, pairs, LANES).transpose(1, 0, 2)

    cd_l = jnp.repeat(cd, RET_HEAD_DIM).reshape(pairs, 1, LANES)
    return decay, per_lane(xi), per_lane(zeta), cd_l


def _ret_call(rq, rk, rv, sg, consts, ng, nb, batch, seq):
    decay, xi, zeta, cd = consts
    pairs = RET_WIDTH // LANES
    blk = pl.BlockSpec((seq, LANES), lambda b, p: (b, p))
    pair3 = lambda r: pl.BlockSpec((1, r, LANES), lambda b, p: (p, 0, 0))
    return pl.pallas_call(
        _ret_kernel, grid=(batch, pairs),
        in_specs=[blk, blk, blk, blk,
                  pl.BlockSpec((2, RET_CHUNK, RET_CHUNK), lambda b, p: (p, 0, 0)),
                  pair3(RET_CHUNK), pair3(RET_CHUNK), pair3(1), pair3(1), pair3(1)],
        out_specs=blk,
        out_shape=jax.ShapeDtypeStruct((batch * seq, RET_WIDTH), BF16),
        compiler_params=pltpu.CompilerParams(dimension_semantics=("arbitrary", "arbitrary")),
        name="retention",
    )(rq, rk, rv, sg, decay, xi, zeta, cd, ng, nb)


def _sortable_key(score):
    bits = pltpu.bitcast(score + 0.0, jnp.int32)
    return jnp.where(bits < 0, bits ^ jnp.int32(0x7FFFFFFF), bits)


def _dsa_kernel(ki_ref, ckv_ref, ckvt_ref, qit_ref, wit_ref, qlt_ref, wuvt_ref, y_ref, key_ref, *, topk):
    qb = Q_BLOCK
    j = pl.program_id(1)
    n_chunks = j + 1
    pairs = IDX_HEADS // 2
    lane_q = lax.broadcasted_iota(jnp.int32, (qb, qb), 1)
    sub_k = lax.broadcasted_iota(jnp.int32, (qb, qb), 0)

    qi_pairs = []
    w_pairs = []
    for p in range(pairs):
        qa = qit_ref[(2 * p) * IDX_DIM:(2 * p + 1) * IDX_DIM, :]
        qb_ = qit_ref[(2 * p + 1) * IDX_DIM:(2 * p + 2) * IDX_DIM, :]
        qi_pairs.append(jnp.concatenate([qa, qb_], axis=1))
        w_pairs.append(jnp.concatenate([wit_ref[2 * p:2 * p + 1, :], wit_ref[2 * p + 1:2 * p + 2, :]], axis=1))

    def score_chunk(c, carry):
        rows = pl.ds(pl.multiple_of(c * qb, qb), qb)
        kic = ki_ref[rows, :]
        acc = jnp.zeros((qb, qb), F32)
        for p in range(pairs):
            rel = jnp.maximum(_dot(kic, qi_pairs[p]), 0.0) * w_pairs[p]
            acc = acc + rel[:, :qb] + rel[:, qb:]
        causal = (c * qb + sub_k) <= (j * qb + lane_q)
        key_ref[rows, :] = jnp.where(causal, _sortable_key(acc), jnp.int32(INT_MIN))
        return carry

    lax.fori_loop(0, n_chunks, score_chunk, 0)

    def count_ge(cand):
        def body(c, acc):
            rows = pl.ds(pl.multiple_of(c * qb, qb), qb)
            hit = jnp.where(key_ref[rows, :] >= cand, 1, 0).astype(jnp.int32)
            return acc + jnp.sum(hit.reshape(qb // 8, 8, qb), axis=0)
        acc = lax.fori_loop(0, n_chunks, body, jnp.zeros((8, qb), jnp.int32))
        return jnp.sum(acc, axis=0, keepdims=True)

    zero_row = jnp.zeros((1, qb), jnp.int32)
    thr = jnp.where(count_ge(zero_row) >= topk, zero_row, jnp.int32(INT_MIN))

    def bit_step(i, thr):
        trial = thr | (jnp.int32(1) << (30 - i))
        return jnp.where(count_ge(trial) >= topk, trial, thr)

    thr = lax.fori_loop(0, 31, bit_step, thr)
    thr = jnp.maximum(thr, jnp.int32(INT_MIN + 1))

    n_ge = count_ge(thr)

    @pl.when(jnp.max(n_ge) > topk)
    def _():
        n_gt = count_ge(thr + 1)
        need = topk - n_gt
        tri = (lax.broadcasted_iota(jnp.int32, (qb, qb), 1) < lax.broadcasted_iota(jnp.int32, (qb, qb), 0)).astype(BF16)

        def body(c, seen):
            rows = pl.ds(pl.multiple_of(c * qb, qb), qb)
            keys = key_ref[rows, :]
            tied = keys == thr
            tied_f = jnp.where(tied, 1.0, 0.0)
            before = seen + _dot(tri, tied_f.astype(BF16))
            drop = tied & (before >= need.astype(F32)) & (n_ge > topk)
            key_ref[rows, :] = jnp.where(drop, jnp.int32(INT_MIN), keys)
            return seen + jnp.sum(tied_f, axis=0, keepdims=True)

        lax.fori_loop(0, n_chunks, body, jnp.zeros((1, qb), F32))

    ql_pairs = []
    for p in range(pairs):
        qa = qlt_ref[(2 * p) * DSA_LATENT:(2 * p + 1) * DSA_LATENT, :]
        qb_ = qlt_ref[(2 * p + 1) * DSA_LATENT:(2 * p + 2) * DSA_LATENT, :]
        ql_pairs.append(jnp.concatenate([qa, qb_], axis=1))

    def attend_chunk(c, carry):
        rows = pl.ds(pl.multiple_of(c * qb, qb), qb)
        ckv = ckv_ref[rows, :]
        ckvt = ckvt_ref[c]
        sel = key_ref[rows, :] >= thr
        sel2 = jnp.concatenate([sel, sel], axis=1)
        new = []
        for p in range(pairs):
            m_old, l_old, acc = carry[p]
            lg = jnp.where(sel2, _dot(ckv, ql_pairs[p]), NEG_BIG)
            m_new = jnp.maximum(m_old, jnp.max(lg, axis=0, keepdims=True))
            alpha = jnp.exp(m_old - m_new)
            pr = jnp.where(sel2, jnp.exp(lg - m_new), 0.0)
            l_new = l_old * alpha + jnp.sum(pr, axis=0, keepdims=True)
            acc = acc * alpha + _dot(ckvt, pr.astype(BF16))
            new.append((m_new, l_new, acc))
        return tuple(new)

    init = tuple((jnp.full((1, 2 * qb), NEG_BIG, F32), jnp.zeros((1, 2 * qb), F32),
                  jnp.zeros((DSA_LATENT, 2 * qb), F32)) for _ in range(pairs))
    final = lax.fori_loop(0, n_chunks, attend_chunk, init)

    outs = []
    for p in range(pairs):
        _, l_fin, acc = final[p]
        o = (acc / l_fin).astype(BF16)
        for hh in range(2):
            outs.append(_dot(wuvt_ref[2 * p + hh], o[:, hh * qb:(hh + 1) * qb]))
    y_t = jnp.concatenate(outs, axis=0)
    y_ref[...] = y_t.T.astype(BF16)


def _dsa_call(ki, ckv, ckvt, qit, wit, qlt, wuvt, batch, seq):
    nb = seq // Q_BLOCK
    topk = min(IDX_TOPK_MAX, seq // 4)
    per_b = lambda w: pl.BlockSpec((seq, w), lambda b, j: (b, 0))
    per_q = lambda r: pl.BlockSpec((r, Q_BLOCK), lambda b, j: (0, b * nb + j))
    return pl.pallas_call(
        functools.partial(_dsa_kernel, topk=topk),
        grid=(batch, nb),
        in_specs=[per_b(IDX_DIM), per_b(DSA_LATENT),
                  pl.BlockSpec((nb, DSA_LATENT, LANES), lambda b, j: (b, 0, 0)),
                  per_q(IDX_HEADS * IDX_DIM), per_q(IDX_HEADS), per_q(DSA_HEADS * DSA_LATENT),
                  _const_spec(wuvt.shape)],
        out_specs=pl.BlockSpec((Q_BLOCK, DSA_WIDTH), lambda b, j: (b * nb + j, 0)),
        out_shape=jax.ShapeDtypeStruct((batch * seq, DSA_WIDTH), BF16),
        scratch_shapes=[pltpu.VMEM((seq, Q_BLOCK), jnp.int32)],
        compiler_params=pltpu.CompilerParams(dimension_semantics=("arbitrary", "arbitrary"),
                                             vmem_limit_bytes=VMEM_LIMIT),
        name="dsa",
    )(ki, ckv, ckvt, qit, wit, qlt, wuvt)


def _rms(a, g):
    return a * lax.rsqrt(jnp.mean(a * a, axis=-1, keepdims=True) + EPS) * g


def _post_kernel(x_ref, yr_ref, yd_ref, p_ref, wo_ref, g1_ref, g2_ref, wg_ref, wu_ref, wd_ref, g3_ref,
                 wpu_ref, wpg_ref, g4_ref, o_ref):
    x = x_ref[...]
    mix = _dot(yr_ref[...], wo_ref[:RET_WIDTH, :]) + _dot(yd_ref[...], wo_ref[RET_WIDTH:, :])
    x = x + _rms(mix, g1_ref[...])
    f = _rms(x, g2_ref[...]).astype(BF16)
    gate = _dot(f, wg_ref[...])
    up = _dot(f, wu_ref[...])
    hid = (gate * jax.nn.sigmoid(gate) * up).astype(BF16)
    x = x + _rms(_dot(hid, wd_ref[...]), g3_ref[...])
    e = _dot(p_ref[...].astype(BF16), wpu_ref[...])
    gt = jax.nn.sigmoid(_dot(x.astype(BF16), wpg_ref[...]))
    o_ref[...] = x + _rms(gt * e, g4_ref[...])


def _post_call(x2, yr, yd, p2, wo, g1, g2, wg, wu, wd, g3, wpu, wpg, g4, tm):
    t = x2.shape[0]
    row = lambda w: pl.BlockSpec((tm, w), lambda i: (i, 0))
    wspec = lambda a: pl.BlockSpec(a.shape, lambda i: (0, 0), pipeline_mode=pl.Buffered(1))
    gspec = _const_spec((1, D_MODEL))
    return pl.pallas_call(
        _post_kernel, grid=(t // tm,),
        in_specs=[row(D_MODEL), row(RET_WIDTH), row(DSA_WIDTH), row(PLE_DIM), wspec(wo), gspec, gspec,
                  wspec(wg), wspec(wu), wspec(wd), gspec, wspec(wpu), wspec(wpg), gspec],
        out_specs=row(D_MODEL),
        out_shape=jax.ShapeDtypeStruct((t, D_MODEL), F32),
        compiler_params=pltpu.CompilerParams(dimension_semantics=("arbitrary",), vmem_limit_bytes=VMEM_LIMIT),
        name="post",
    )(x2, yr, yd, p2, wo, g1, g2, wg, wu, wd, g3, wpu, wpg, g4)


def kernel(x, p, positions, pre_mix_g, w_in, ret_norm_g, ret_norm_b, kv_norm_g, idx_k_norm_g, idx_k_norm_b, w_uk,
           w_uv, w_out, post_mix_g, pre_ffn_g, w_ffn_gate, w_ffn_up, w_ffn_down, post_ffn_g, w_ple_up, w_ple_gate,
           ple_norm_g):
    batch, seq, _ = x.shape
    t = batch * seq
    depth = w_in.shape[0]
    tm = min(t, 512)
    x2 = x.reshape(t, D_MODEL)
    cos, sin = _rope_tables(positions)
    ret_consts = _ret_consts()
    pairs = RET_WIDTH // LANES

    for i in range(depth):
        w = w_in[i]
        wt = jnp.concatenate([w[:, :_OFF_AQ], w[:, _OFF_AC:_OFF_IQ], w[:, _OFF_IK:_OFF_IW],
                              jnp.zeros((D_MODEL, LANES - IDX_DIM), w.dtype)], axis=1).astype(BF16)
        wf = jnp.concatenate([w[:, _OFF_AQ:_OFF_AC], w[:, _OFF_IQ:_OFF_IK]], axis=1).T.astype(BF16)
        wfw = jnp.concatenate([w[:, _OFF_IW:], jnp.zeros((D_MODEL, 16 - IDX_HEADS), w.dtype)], axis=1).T.astype(BF16)
        wukt = jnp.swapaxes(w_uk[i], 1, 2).astype(BF16)
        wuvt = jnp.swapaxes(w_uv[i], 1, 2).astype(BF16)

        rq, rk, rv, sg, ckv, ckvt, ki, qit, wit, qlt = _proj_call(
            x2, pre_mix_g[i].reshape(1, -1), wt, wf, wfw, cos, sin, kv_norm_g[i].reshape(1, -1),
            idx_k_norm_g[i].reshape(1, -1), idx_k_norm_b[i].reshape(1, -1), wukt, tm)
        y_ret = _ret_call(rq, rk, rv, sg, ret_consts, ret_norm_g[i].reshape(pairs, 1, LANES),
                          ret_norm_b[i].reshape(pairs, 1, LANES), batch, seq)
        y_dsa = _dsa_call(ki, ckv, ckvt, qit, wit, qlt, wuvt, batch, seq)
        x2 = _post_call(
            x2, y_ret, y_dsa, p[i].reshape(t, PLE_DIM), w_out[i].astype(BF16), post_mix_g[i].reshape(1, -1),
            pre_ffn_g[i].reshape(1, -1), w_ffn_gate[i].astype(BF16), w_ffn_up[i].astype(BF16),
            w_ffn_down[i].astype(BF16), post_ffn_g[i].reshape(1, -1), w_ple_up[i].astype(BF16),
            w_ple_gate[i].astype(BF16), ple_norm_g[i].reshape(1, -1), tm)
    return x2.reshape(batch, seq, D_MODEL)
```

```python
import functools

import jax
import jax.numpy as jnp
from jax import lax
from jax.experimental import pallas as pl
from jax.experimental.pallas import tpu as pltpu

D_MODEL = 1024
RET_HEADS = 8
RET_HEAD_DIM = 64
RET_WIDTH = RET_HEADS * RET_HEAD_DIM
RET_CHUNK = 128
ROPE_BASE = 10000.0
DSA_HEADS = 8
DSA_HEAD_DIM = 64
DSA_WIDTH = DSA_HEADS * DSA_HEAD_DIM
DSA_LATENT = 128
IDX_HEADS = 8
IDX_DIM = 64
IDX_TOPK_MAX = 256
Q_BLOCK = 128
PLE_DIM = 256
D_FF = 2816
EPS = 1e-6

LANES = 128
SUBLANES = 8
BF16_ROWS = 16
VMEM_LIMIT = 56 * 1024 * 1024
INT_MIN = -(2 ** 31)
NEG_BIG = -1e30

F32 = jnp.float32
BF16 = jnp.bfloat16

_OFF_RET = 0
_OFF_AQ = 4 * RET_WIDTH
_OFF_AC = _OFF_AQ + DSA_WIDTH
_OFF_IQ = _OFF_AC + DSA_LATENT
_OFF_IK = _OFF_IQ + IDX_HEADS * IDX_DIM
_OFF_IW = _OFF_IK + IDX_DIM
_TOK_COLS = 4 * RET_WIDTH + DSA_LATENT + LANES
_PAIRS = DSA_HEADS // 2
KEY_STEP = 256
assert IDX_HEADS == DSA_HEADS and Q_BLOCK == LANES and KEY_STEP == 2 * Q_BLOCK


def _nt_dot(a, b):
    return lax.dot_general(a, b, (((1,), (1,)), ((), ())), preferred_element_type=F32)


def _dot(a, b):
    return jnp.dot(a, b, preferred_element_type=F32)


def _tree_sum(xs):
    while len(xs) > 1:
        xs = [xs[i] + xs[i + 1] for i in range(0, len(xs) - 1, 2)] + ([xs[-1]] if len(xs) % 2 else [])
    return xs[0]


def _const_spec(shape):
    return pl.BlockSpec(shape, lambda *_: (0,) * len(shape))


def _rope_kernel(pos_ref, freq_ref, sign_ref, cos_ref, sin_ref):
    ang = pos_ref[...].astype(F32) * freq_ref[...]
    cos_ref[...] = jnp.cos(ang)
    sin_ref[...] = jnp.sin(ang) * sign_ref[...]


def _rope_tables(positions):
    t = positions.size
    tm = min(t, 1024)
    d = RET_HEAD_DIM
    inv_freq = 1.0 / (ROPE_BASE ** (jnp.arange(0, d, 2, dtype=F32) / d))
    freq = jnp.tile(inv_freq, LANES // (d // 2)).reshape(1, LANES)
    sign = jnp.tile(jnp.concatenate([-jnp.ones(d // 2, F32), jnp.ones(d // 2, F32)]), LANES // d).reshape(1, LANES)
    return pl.pallas_call(
        _rope_kernel,
        grid=(t // tm,),
        in_specs=[pl.BlockSpec((tm, 1), lambda i: (i, 0)), _const_spec((1, LANES)), _const_spec((1, LANES))],
        out_specs=[pl.BlockSpec((tm, LANES), lambda i: (i, 0))] * 2,
        out_shape=[jax.ShapeDtypeStruct((t, LANES), F32)] * 2,
        name="rope_tables",
    )(positions.reshape(t, 1), freq, sign)


def _proj_kernel(x_ref, g_ref, wt_ref, wf_ref, wfw_ref, cos_ref, sin_ref, kvg_ref, ikg_ref, ikb_ref, wukt_ref,
                 rq_ref, rk_ref, rv_ref, rg_ref, ckv_ref, ckvt_ref, ki_ref, qip_ref, wip_ref, qlp_ref):
    tm = x_ref.shape[0]
    x = x_ref[...]
    h = (x * lax.rsqrt(jnp.mean(x * x, axis=-1, keepdims=True) + EPS) * g_ref[...]).astype(BF16)
    pt = _dot(h, wt_ref[...])
    pf = _nt_dot(wf_ref[...], h)
    pw = _nt_dot(wfw_ref[...], h)

    cos = cos_ref[...]
    sin = sin_ref[...]
    lane = lax.broadcasted_iota(jnp.int32, (tm, LANES), 1)
    first_half = (lane % RET_HEAD_DIM) < (RET_HEAD_DIM // 2)

    def rotate(xg):
        partner = jnp.where(first_half, pltpu.roll(xg, LANES - RET_HEAD_DIM // 2, 1),
                            pltpu.roll(xg, RET_HEAD_DIM // 2, 1))
        return xg * cos + partner * sin

    for g in range(RET_WIDTH // LANES):
        sl = slice(g * LANES, (g + 1) * LANES)
        rq_ref[:, sl] = rotate(pt[:, g * LANES:(g + 1) * LANES]).astype(BF16)
        rk = rotate(pt[:, RET_WIDTH + g * LANES:RET_WIDTH + (g + 1) * LANES])
        rk_ref[:, sl] = (rk * (RET_HEAD_DIM ** -0.5)).astype(BF16)
    rv_ref[...] = pt[:, 2 * RET_WIDTH:3 * RET_WIDTH].astype(BF16)
    rg = pt[:, 3 * RET_WIDTH:4 * RET_WIDTH]
    rg_ref[...] = (rg * jax.nn.sigmoid(rg)).astype(BF16)

    ac = pt[:, 4 * RET_WIDTH:4 * RET_WIDTH + DSA_LATENT]
    ckv = ac * lax.rsqrt(jnp.mean(ac * ac, axis=-1, keepdims=True) + EPS) * kvg_ref[...]
    ckv_ref[...] = ckv.astype(BF16)
    for c in range(tm // LANES):
        ckvt_ref[c] = ckv[c * LANES:(c + 1) * LANES, :].T.astype(BF16)

    ik = pt[:, 4 * RET_WIDTH + DSA_LATENT:4 * RET_WIDTH + DSA_LATENT + IDX_DIM]
    mu = jnp.mean(ik, axis=-1, keepdims=True)
    var = jnp.mean(jnp.square(ik - mu), axis=-1, keepdims=True)
    ki_ref[...] = ((ik - mu) * lax.rsqrt(var + EPS) * ikg_ref[...] + ikb_ref[...]).astype(BF16)

    wi = pw[:IDX_HEADS, :] * ((IDX_HEADS ** -0.5) * (IDX_DIM ** -0.5))
    for hh in range(DSA_HEADS):
        p, half = hh // 2, slice((hh % 2) * Q_BLOCK, (hh % 2 + 1) * Q_BLOCK)
        qi = pf[DSA_WIDTH + hh * IDX_DIM:DSA_WIDTH + (hh + 1) * IDX_DIM, :].astype(BF16)
        aqt = pf[hh * DSA_HEAD_DIM:(hh + 1) * DSA_HEAD_DIM, :].astype(BF16)
        ql = (_dot(wukt_ref[hh], aqt) * (DSA_HEAD_DIM ** -0.5)).astype(BF16)
        for c in range(tm // Q_BLOCK):
            cols = slice(c * Q_BLOCK, (c + 1) * Q_BLOCK)
            qip_ref[c, p, :, half] = qi[:, cols]
            qlp_ref[c, p, :, half] = ql[:, cols]
            wip_ref[c, p, :, half] = wi[hh:hh + 1, cols]


def _proj_call(x2, g, wt, wf, wfw, cos, sin, kvg, ikg, ikb, wukt, tm):
    t = x2.shape[0]
    nqb = t // Q_BLOCK
    row = lambda w: pl.BlockSpec((tm, w), lambda i: (i, 0))
    out_shape = [
        jax.ShapeDtypeStruct((t, RET_WIDTH), BF16),
        jax.ShapeDtypeStruct((t, RET_WIDTH), BF16),
        jax.ShapeDtypeStruct((t, RET_WIDTH), BF16),
        jax.ShapeDtypeStruct((t, RET_WIDTH), BF16),
        jax.ShapeDtypeStruct((t, DSA_LATENT), BF16),
        jax.ShapeDtypeStruct((t // LANES, DSA_LATENT, LANES), BF16),
        jax.ShapeDtypeStruct((t, IDX_DIM), BF16),
        jax.ShapeDtypeStruct((nqb, _PAIRS, IDX_DIM, 2 * Q_BLOCK), BF16),
        jax.ShapeDtypeStruct((nqb, _PAIRS, 1, 2 * Q_BLOCK), F32),
        jax.ShapeDtypeStruct((nqb, _PAIRS, DSA_LATENT, 2 * Q_BLOCK), BF16),
    ]
    pair_spec = lambda r: pl.BlockSpec((tm // Q_BLOCK, _PAIRS, r, 2 * Q_BLOCK), lambda i: (i, 0, 0, 0))
    out_specs = [row(RET_WIDTH)] * 4 + [
        row(DSA_LATENT),
        pl.BlockSpec((tm // LANES, DSA_LATENT, LANES), lambda i: (i, 0, 0)),
        row(IDX_DIM),
        pair_spec(IDX_DIM),
        pair_spec(1),
        pair_spec(DSA_LATENT),
    ]
    in_specs = [
        row(D_MODEL), _const_spec((1, D_MODEL)), _const_spec(wt.shape), _const_spec(wf.shape), _const_spec(wfw.shape),
        row(LANES), row(LANES), _const_spec((1, DSA_LATENT)), _const_spec((1, IDX_DIM)), _const_spec((1, IDX_DIM)),
        _const_spec(wukt.shape),
    ]
    return pl.pallas_call(
        _proj_kernel, grid=(t // tm,), in_specs=in_specs, out_specs=out_specs, out_shape=out_shape,
        compiler_params=pltpu.CompilerParams(dimension_semantics=("arbitrary",), vmem_limit_bytes=VMEM_LIMIT),
        name="in_proj",
    )(x2, g, wt, wf, wfw, cos, sin, kvg, ikg, ikb, wukt)


def _ret_kernel(q_ref, k_ref, v_ref, sg_ref, decay_ref, xi_ref, zeta_ref, cd_ref, ng_ref, nb_ref, o_ref):
    c = RET_CHUNK
    n_chunks = q_ref.shape[0] // c
    lane = lax.broadcasted_iota(jnp.int32, (c, LANES), 1)
    head0 = lane < RET_HEAD_DIM
    row = lax.broadcasted_iota(jnp.int32, (LANES, LANES), 0)
    col = lax.broadcasted_iota(jnp.int32, (LANES, LANES), 1)
    same_head = (row < RET_HEAD_DIM) == (col < RET_HEAD_DIM)
    d0 = decay_ref[0]
    d1 = decay_ref[1]
    xi = xi_ref[0]
    zeta = zeta_ref[0]
    cd = cd_ref[0]
    ng = ng_ref[0]
    nb = nb_ref[0]
    inv_d = 1.0 / RET_HEAD_DIM

    def head_stat(a):
        s0 = jnp.sum(jnp.where(head0, a, 0.0), axis=-1, keepdims=True)
        s1 = jnp.sum(jnp.where(head0, 0.0, a), axis=-1, keepdims=True)
        return jnp.where(head0, s0, s1) * inv_d

    def body(n, r_state):
        rows = pl.ds(pl.multiple_of(n * c, c), c)
        q = q_ref[rows, :]
        k = k_ref[rows, :]
        v = v_ref[rows, :]
        zero = jnp.zeros_like(q)
        s0 = _nt_dot(jnp.where(head0, q, zero), k) * d0
        s1 = _nt_dot(jnp.where(head0, zero, q), k) * d1
        sc = jnp.concatenate([s0, s1], axis=1).astype(BF16)
        vm = jnp.concatenate([jnp.where(head0, v, zero), jnp.where(head0, zero, v)], axis=0)
        intra = _dot(sc, vm)
        cross = _dot((q.astype(F32) * xi).astype(BF16), r_state.astype(BF16))
        o = intra + cross
        mu = head_stat(o)
        dlt = o - mu
        var = head_stat(dlt * dlt)
        y = dlt * lax.rsqrt(var + EPS) * ng + nb
        o_ref[rows, :] = (sg_ref[rows, :].astype(F32) * y).astype(BF16)
        kz = (k.astype(F32) * zeta).T.astype(BF16)
        kv = jnp.where(same_head, _dot(kz, v), 0.0)
        return cd * r_state + kv

    lax.fori_loop(0, n_chunks, body, jnp.zeros((LANES, LANES), F32))


def _ret_consts():
    h, c = RET_HEADS, RET_CHUNK
    log_g = jnp.log1p(-jnp.power(2.0, -5.0 - jnp.arange(h, dtype=F32)))
    i = jnp.arange(c, dtype=F32)
    diff = i[:, None] - i[None, :]
    decay = jnp.where(diff >= 0, jnp.exp(log_g[:, None, None] * jnp.maximum(diff, 0.0)), 0.0)
    zeta = jnp.exp(log_g[None, :] * (c - 1 - i)[:, None])
    xi = jnp.exp(log_g[None, :] * (i + 1.0)[:, None])
    cd = jnp.exp(log_g * c)
    pairs = h * RET_HEAD_DIM // LANES

    def per_lane(a):
        return jnp.repeat(a, RET_HEAD_DIM, axis=1).reshape(c, pairs, LANES).transpose(1, 0, 2)

    cd_l = jnp.repeat(cd, RET_HEAD_DIM).reshape(pairs, 1, LANES)
    return decay, per_lane(xi), per_lane(zeta), cd_l


def _ret_call(rq, rk, rv, sg, consts, ng, nb, batch, seq):
    decay, xi, zeta, cd = consts
    pairs = RET_WIDTH // LANES
    blk = pl.BlockSpec((seq, LANES), lambda b, p: (b, p))
    pair3 = lambda r: pl.BlockSpec((1, r, LANES), lambda b, p: (p, 0, 0))
    return pl.pallas_call(
        _ret_kernel, grid=(batch, pairs),
        in_specs=[blk, blk, blk, blk,
                  pl.BlockSpec((2, RET_CHUNK, RET_CHUNK), lambda b, p: (p, 0, 0)),
                  pair3(RET_CHUNK), pair3(RET_CHUNK), pair3(1), pair3(1), pair3(1)],
        out_specs=blk,
        out_shape=jax.ShapeDtypeStruct((batch * seq, RET_WIDTH), BF16),
        compiler_params=pltpu.CompilerParams(dimension_semantics=("arbitrary", "arbitrary")),
        name="retention",
    )(rq, rk, rv, sg, decay, xi, zeta, cd, ng, nb)


_KEY_OF_NEG_INF = 0x807FFFFF - 2 ** 32


def _fine_threshold(key):
    bits = jnp.where(key < 0, key ^ jnp.int32(0x7FFFFFFF), key)
    return jnp.where(key < _KEY_OF_NEG_INF, -jnp.inf, pltpu.bitcast(bits, F32))


def _coarse_threshold(key):
    bits = jnp.where(key < 0, key ^ jnp.int32(0x7FFF0000), key)
    return jnp.where(key < (_KEY_OF_NEG_INF & ~0xFFFF), -jnp.inf, pltpu.bitcast(bits, F32))


def _dsa_kernel(ki_ref, ckv_ref, ckvt_ref, qip_ref, wip_ref, qlp_ref, wuvt_ref, y_ref,
                sc_ref, sb_ref, acc_ref, lg_ref, bias_ref, *, topk):
    qb, ks = Q_BLOCK, KEY_STEP
    j = pl.program_id(1)
    n_steps = (j + 2) // 2
    sub_k = lax.broadcasted_iota(jnp.int32, (ks, qb), 0)
    q_pos = j * qb + lax.broadcasted_iota(jnp.int32, (1, qb), 1)
    topk_f = jnp.float32(topk)

    def step_rows(s):
        return pl.ds(pl.multiple_of(s * ks, ks), ks)

    def causal_mask(s):
        return sub_k <= (q_pos - s * ks)

    def score_step(s, carry):
        rows = step_rows(s)
        kis = ki_ref[rows, :]
        acc = jnp.zeros((ks, qb), F32)
        for p in range(_PAIRS):
            rel = jnp.maximum(_dot(kis, qip_ref[0, p]), 0.0) * wip_ref[0, p]
            acc = acc + rel[:, :qb] + rel[:, qb:]
        sc = jnp.where(causal_mask(s), acc, -jnp.inf)
        sc_ref[rows, :] = sc
        hi = pltpu.bitcast(sc, jnp.int32) & jnp.int32(-65536)
        sb_ref[rows, :] = pltpu.bitcast(hi, F32).astype(BF16)
        return carry

    lax.fori_loop(0, n_steps, score_step, 0)

    one_b = jnp.ones((BF16_ROWS, qb), BF16)
    zero_b = jnp.zeros((BF16_ROWS, qb), BF16)

    def count_coarse(g):
        gb = jnp.broadcast_to(g, (BF16_ROWS, qb)).astype(BF16)

        def body(s, acc):
            x = sb_ref[step_rows(s), :]
            hits = [jnp.where(x[r * BF16_ROWS:(r + 1) * BF16_ROWS] >= gb, one_b, zero_b)
                    for r in range(ks // BF16_ROWS)]
            return acc + _tree_sum(hits).astype(F32)

        acc = lax.fori_loop(0, n_steps, body, jnp.zeros((BF16_ROWS, qb), F32))
        return jnp.sum(acc, axis=0, keepdims=True)

    def count_fine(f, strict=False):
        f8 = jnp.broadcast_to(f, (SUBLANES, qb))

        def body(s, acc):
            x = sc_ref[step_rows(s), :]
            hits = []
            for r in range(ks // SUBLANES):
                xr = x[r * SUBLANES:(r + 1) * SUBLANES]
                hits.append(jnp.where((xr > f8) if strict else (xr >= f8), 1.0, 0.0))
            return acc + _tree_sum(hits)

        acc = lax.fori_loop(0, n_steps, body, jnp.zeros((SUBLANES, qb), F32))
        return jnp.sum(acc, axis=0, keepdims=True)

    def search_step(count, to_threshold):
        def step(i, cand):
            bit = jnp.int32(1) << (31 - i)
            trial = jnp.where(i == 0, cand ^ bit, cand | bit)
            return jnp.where(count(to_threshold(trial)) >= topk_f, trial, cand)
        return step

    cand = jnp.full((1, qb), INT_MIN, jnp.int32)
    cand = lax.fori_loop(0, 16, search_step(count_coarse, _coarse_threshold), cand)
    cand = lax.fori_loop(16, 32, search_step(count_fine, _fine_threshold), cand)
    thr = _fine_threshold(cand)

    excess = (count_fine(thr) > topk_f) & (thr > -jnp.inf)

    @pl.when(jnp.max(jnp.where(excess, 1, 0)) > 0)
    def _():
        need = topk_f - count_fine(thr, strict=True)
        earlier = (lax.broadcasted_iota(jnp.int32, (ks, ks), 1)
                   < lax.broadcasted_iota(jnp.int32, (ks, ks), 0)).astype(BF16)

        def body(s, seen):
            rows = step_rows(s)
            x = sc_ref[rows, :]
            tied = x == thr
            tied_f = jnp.where(tied, 1.0, 0.0)
            before = seen + _dot(earlier, tied_f.astype(BF16))
            drop = tied & (before >= need) & excess
            sc_ref[rows, :] = jnp.where(drop, -jnp.inf, x)
            return seen + jnp.sum(tied_f, axis=0, keepdims=True)

        lax.fori_loop(0, n_steps, body, jnp.zeros((1, qb), F32))

    acc_ref[...] = jnp.zeros_like(acc_ref)

    def attend_step(s, carry):
        rows = step_rows(s)
        ckv = ckv_ref[rows, :]
        ckvt = jnp.concatenate([ckvt_ref[2 * s], ckvt_ref[2 * s + 1]], axis=1)
        sel = (sc_ref[rows, :] >= thr) & causal_mask(s)
        bias_ref[...] = jnp.where(sel, 0.0, NEG_BIG)
        for p in range(_PAIRS):
            lg_ref[p] = _dot(ckv, qlp_ref[0, p])
        new = []
        for p in range(_PAIRS):
            prs, alphas = [], []
            for hh in range(2):
                m_old, l_old = carry[2 * p + hh]
                lgh = lg_ref[p, :, hh * qb:(hh + 1) * qb] + bias_ref[...]
                m_new = jnp.maximum(m_old, jnp.max(lgh, axis=0, keepdims=True))
                alpha = jnp.exp(m_old - m_new)
                pr = jnp.exp(lgh - m_new)
                new.append((m_new, l_old * alpha + jnp.sum(pr, axis=0, keepdims=True)))
                prs.append(pr.astype(BF16))
                alphas.append(alpha)
            acc_ref[p] = (acc_ref[p] * jnp.concatenate(alphas, axis=1)
                          + _dot(ckvt, jnp.concatenate(prs, axis=1)))
        return tuple(new)

    init = tuple((jnp.full((1, qb), NEG_BIG, F32), jnp.zeros((1, qb), F32)) for _ in range(DSA_HEADS))
    final = lax.fori_loop(0, n_steps, attend_step, init)

    outs = []
    for hh in range(DSA_HEADS):
        inv_l = 1.0 / final[hh][1]
        o = (acc_ref[hh // 2, :, (hh % 2) * qb:(hh % 2 + 1) * qb] * inv_l).astype(BF16)
        outs.append(_dot(wuvt_ref[hh], o))
    y_ref[...] = jnp.concatenate(outs, axis=0).T.astype(BF16)


def _dsa_call(ki, ckv, ckvt, qip, wip, qlp, wuvt, batch, seq):
    assert seq % KEY_STEP == 0
    nb = seq // Q_BLOCK
    topk = min(IDX_TOPK_MAX, seq // 4)
    per_b = lambda w: pl.BlockSpec((seq, w), lambda b, j: (b, 0))
    per_q = lambda r: pl.BlockSpec((1, _PAIRS, r, 2 * Q_BLOCK), lambda b, j: (b * nb + j, 0, 0, 0))
    return pl.pallas_call(
        functools.partial(_dsa_kernel, topk=topk),
        grid=(batch, nb),
        in_specs=[per_b(IDX_DIM), per_b(DSA_LATENT),
                  pl.BlockSpec((nb, DSA_LATENT, LANES), lambda b, j: (b, 0, 0)),
                  per_q(IDX_DIM), per_q(1), per_q(DSA_LATENT),
                  _const_spec(wuvt.shape)],
        out_specs=pl.BlockSpec((Q_BLOCK, DSA_WIDTH), lambda b, j: (b * nb + j, 0)),
        out_shape=jax.ShapeDtypeStruct((batch * seq, DSA_WIDTH), BF16),
        scratch_shapes=[pltpu.VMEM((seq, Q_BLOCK), F32),
                        pltpu.VMEM((seq, Q_BLOCK), BF16),
                        pltpu.VMEM((_PAIRS, DSA_LATENT, 2 * Q_BLOCK), F32),
                        pltpu.VMEM((_PAIRS, KEY_STEP, 2 * Q_BLOCK), F32),
                        pltpu.VMEM((KEY_STEP, Q_BLOCK), F32)],
        compiler_params=pltpu.CompilerParams(dimension_semantics=("arbitrary", "arbitrary"),
                                             vmem_limit_bytes=VMEM_LIMIT),
        name="dsa",
    )(ki, ckv, ckvt, qip, wip, qlp, wuvt)


def _rms(a, g):
    return a * lax.rsqrt(jnp.mean(a * a, axis=-1, keepdims=True) + EPS) * g


def _post_kernel(x_ref, yr_ref, yd_ref, p_ref, wo_ref, g1_ref, g2_ref, wg_ref, wu_ref, wd_ref, g3_ref,
                 wpu_ref, wpg_ref, g4_ref, o_ref):
    x = x_ref[...]
    mix = _dot(yr_ref[...], wo_ref[:RET_WIDTH, :]) + _dot(yd_ref[...], wo_ref[RET_WIDTH:, :])
    x = x + _rms(mix, g1_ref[...])
    f = _rms(x, g2_ref[...]).astype(BF16)
    gate = _dot(f, wg_ref[...])
    up = _dot(f, wu_ref[...])
    hid = (gate * jax.nn.sigmoid(gate) * up).astype(BF16)
    x = x + _rms(_dot(hid, wd_ref[...]), g3_ref[...])
    e = _dot(p_ref[...].astype(BF16), wpu_ref[...])
    gt = jax.nn.sigmoid(_dot(x.astype(BF16), wpg_ref[...]))
    o_ref[...] = x + _rms(gt * e, g4_ref[...])


def _post_call(x2, yr, yd, p2, wo, g1, g2, wg, wu, wd, g3, wpu, wpg, g4, tm):
    t = x2.shape[0]
    row = lambda w: pl.BlockSpec((tm, w), lambda i: (i, 0))
    wspec = lambda a: pl.BlockSpec(a.shape, lambda i: (0, 0), pipeline_mode=pl.Buffered(1))
    gspec = _const_spec((1, D_MODEL))
    return pl.pallas_call(
        _post_kernel, grid=(t // tm,),
        in_specs=[row(D_MODEL), row(RET_WIDTH), row(DSA_WIDTH), row(PLE_DIM), wspec(wo), gspec, gspec,
                  wspec(wg), wspec(wu), wspec(wd), gspec, wspec(wpu), wspec(wpg), gspec],
        out_specs=row(D_MODEL),
        out_shape=jax.ShapeDtypeStruct((t, D_MODEL), F32),
        compiler_params=pltpu.CompilerParams(dimension_semantics=("arbitrary",), vmem_limit_bytes=VMEM_LIMIT),
        name="post",
    )(x2, yr, yd, p2, wo, g1, g2, wg, wu, wd, g3, wpu, wpg, g4)


def kernel(x, p, positions, pre_mix_g, w_in, ret_norm_g, ret_norm_b, kv_norm_g, idx_k_norm_g, idx_k_norm_b, w_uk,
           w_uv, w_out, post_mix_g, pre_ffn_g, w_ffn_gate, w_ffn_up, w_ffn_down, post_ffn_g, w_ple_up, w_ple_gate,
           ple_norm_g):
    batch, seq, _ = x.shape
    t = batch * seq
    depth = w_in.shape[0]
    tm = min(t, 512)
    x2 = x.reshape(t, D_MODEL)
    cos, sin = _rope_tables(positions)
    ret_consts = _ret_consts()
    pairs = RET_WIDTH // LANES

    for i in range(depth):
        w = w_in[i]
        wt = jnp.concatenate([w[:, :_OFF_AQ], w[:, _OFF_AC:_OFF_IQ], w[:, _OFF_IK:_OFF_IW],
                              jnp.zeros((D_MODEL, LANES - IDX_DIM), w.dtype)], axis=1).astype(BF16)
        wf = jnp.concatenate([w[:, _OFF_AQ:_OFF_AC], w[:, _OFF_IQ:_OFF_IK]], axis=1).T.astype(BF16)
        wfw = jnp.concatenate([w[:, _OFF_IW:], jnp.zeros((D_MODEL, 16 - IDX_HEADS), w.dtype)], axis=1).T.astype(BF16)
        wukt = jnp.swapaxes(w_uk[i], 1, 2).astype(BF16)
        wuvt = jnp.swapaxes(w_uv[i], 1, 2).astype(BF16)

        rq, rk, rv, sg, ckv, ckvt, ki, qip, wip, qlp = _proj_call(
            x2, pre_mix_g[i].reshape(1, -1), wt, wf, wfw, cos, sin, kv_norm_g[i].reshape(1, -1),
            idx_k_norm_g[i].reshape(1, -1), idx_k_norm_b[i].reshape(1, -1), wukt, tm)
        y_ret = _ret_call(rq, rk, rv, sg, ret_consts, ret_norm_g[i].reshape(pairs, 1, LANES),
                          ret_norm_b[i].reshape(pairs, 1, LANES), batch, seq)
        y_dsa = _dsa_call(ki, ckv, ckvt, qip, wip, qlp, wuvt, batch, seq)
        x2 = _post_call(
            x2, y_ret, y_dsa, p[i].reshape(t, PLE_DIM), w_out[i].astype(BF16), post_mix_g[i].reshape(1, -1),
            pre_ffn_g[i].reshape(1, -1), w_ffn_gate[i].astype(BF16), w_ffn_up[i].astype(BF16),
            w_ffn_down[i].astype(BF16), post_ffn_g[i].reshape(1, -1), w_ple_up[i].astype(BF16),
            w_ple_gate[i].astype(BF16), ple_norm_g[i].reshape(1, -1), tm)
    return x2.reshape(batch, seq, D_MODEL)
```
